```python
import jax, jax.numpy as jnp
from jax import lax
import numpy as np

D_MODEL = 1024
BATCH = 2
SEQ = 16384
DEPTH = 1
DEC_BATCH = 1
DEC_SEQ = 16384
PAST_LEN = 128

MIX_WIDTH = D_MODEL
FNET_WIDTH = MIX_WIDTH // 2
LRU_WIDTH = MIX_WIDTH - FNET_WIDTH
FNET_GROUPS = 8
FNET_GROUP_DIM = FNET_WIDTH // FNET_GROUPS
LRU_HEADS = 8
LRU_HEAD_DIM = LRU_WIDTH // LRU_HEADS
CONV_WIDTH = 4
CONV_LEFT = CONV_WIDTH // 2
RGLRU_C = 8.0
IN_WIDTH = FNET_WIDTH + 2 * LRU_WIDTH
PEER_HEADS = 8
PEER_NKEYS = 128
PEER_EXPERTS = PEER_NKEYS * PEER_NKEYS
PEER_KEY_DIM = 128
PEER_TOPK = 16
PEER_CHUNK = 128
NORM_EPS = 1e-6

kernel_name = "hybrid_fnet_rglru_peer_encoder"


def rmsnorm(x, g):
    xf = x.astype(jnp.float32)
    y = xf * lax.rsqrt(jnp.mean(xf * xf, axis=-1, keepdims=True) + NORM_EPS)
    return (y * g.astype(jnp.float32)).astype(x.dtype)


def fnet_mixer(u, fnet_w):
    B, S, _ = u.shape
    ug = u.reshape(B, S, FNET_GROUPS, FNET_GROUP_DIM).astype(jnp.float32)
    f = jnp.fft.fft2(ug, axes=(1, 3), norm="ortho").real.astype(u.dtype)
    y = jnp.einsum('bsgc,gcd->bsgd', f, fnet_w)
    return y.reshape(B, S, FNET_WIDTH)


def centred_conv(u, conv_w, conv_b):
    S = u.shape[1]
    up = jnp.pad(u, ((0, 0), (CONV_LEFT, CONV_WIDTH - 1 - CONV_LEFT), (0, 0)))
    y = up[:, 0:S] * conv_w[0]
    for k in range(1, CONV_WIDTH):
        y = y + up[:, k:k + S] * conv_w[k]
    return y + conv_b


def _lin_combine(left, right):
    a1, b1 = left
    a2, b2 = right
    return a1 * a2, a2 * b1 + b2


def rglru_direction(xr, wa, ba, wx, bx, lam, reverse):
    B, S, W = xr.shape
    xh = xr.reshape(B, S, LRU_HEADS, LRU_HEAD_DIM)
    r = jax.nn.sigmoid(jnp.einsum('bshi,hij->bshj', xh, wa) + ba).reshape(B, S, W).astype(jnp.float32)
    i = jax.nn.sigmoid(jnp.einsum('bshi,hij->bshj', xh, wx) + bx).reshape(B, S, W).astype(jnp.float32)
    log_a = -RGLRU_C * jax.nn.softplus(-lam.astype(jnp.float32)) * r
    a = jnp.exp(log_a)
    b = jnp.sqrt(-jnp.expm1(2.0 * log_a)) * i * xr.astype(jnp.float32)
    _, h = lax.associative_scan(_lin_combine, (a, b), axis=1, reverse=reverse)
    return h


def peer_ffn(h, peer_wq, peer_k1, peer_k2, peer_u, peer_v):
    B, S, D = h.shape
    n_chunks = S // PEER_CHUNK
    hc = h.reshape(B, n_chunks, PEER_CHUNK, D).transpose(1, 0, 2, 3)

    def one_chunk(hb):
        q = (hb @ peer_wq).reshape(B, PEER_CHUNK, PEER_HEADS, 2, PEER_KEY_DIM)
        s1 = jnp.einsum('bchd,hnd->bchn', q[..., 0, :], peer_k1)
        s2 = jnp.einsum('bchd,hnd->bchn', q[..., 1, :], peer_k2)
        t1, i1 = lax.top_k(s1, PEER_TOPK)
        t2, i2 = lax.top_k(s2, PEER_TOPK)
        cand = (t1[..., :, None] + t2[..., None, :]).reshape(B, PEER_CHUNK, PEER_HEADS, PEER_TOPK * PEER_TOPK)
        cidx = (i1[..., :, None] * PEER_NKEYS + i2[..., None, :]).reshape(B, PEER_CHUNK, PEER_HEADS, PEER_TOPK * PEER_TOPK)
        top, pos = lax.top_k(cand, PEER_TOPK)
        eidx = jnp.take_along_axis(cidx, pos, axis=-1)
        gw = jax.nn.softmax(top.astype(jnp.float32), axis=-1).astype(hb.dtype)
        u_sel = jnp.take(peer_u, eidx, axis=0)
        act = jax.nn.gelu(jnp.einsum('bcd,bchkd->bchk', hb, u_sel))
        v_sel = jnp.take(peer_v, eidx, axis=0)
        return jnp.einsum('bchk,bchkd->bcd', gw * act, v_sel)

    out = lax.map(one_chunk, hc)
    return out.transpose(1, 0, 2, 3).reshape(B, S, D)


def encoder_layer(x, c, ada_w, ada_b, norm1_g, w_in, fnet_w, conv_w, conv_b, lru_wa, lru_ba, lru_wx, lru_bx, lru_lambda, w_out, norm2_g, peer_wq, peer_k1, peer_k2, peer_u, peer_v):
    mod = (jax.nn.silu(c) @ ada_w + ada_b)[:, None, :]
    sh1, sc1, g1, sh2, sc2, g2 = jnp.split(mod, 6, axis=-1)
    h = rmsnorm(x, norm1_g) * (1.0 + sc1) + sh1
    p = h @ w_in
    u_f = p[..., :FNET_WIDTH]
    u_x = p[..., FNET_WIDTH:FNET_WIDTH + LRU_WIDTH]
    u_g = p[..., FNET_WIDTH + LRU_WIDTH:]
    y_f = fnet_mixer(u_f, fnet_w)
    xr = centred_conv(u_x, conv_w, conv_b)
    h_l = (rglru_direction(xr, lru_wa[0], lru_ba[0], lru_wx[0], lru_bx[0], lru_lambda[0], False)
           + rglru_direction(xr, lru_wa[1], lru_ba[1], lru_wx[1], lru_bx[1], lru_lambda[1], True))
    y_l = h_l.astype(x.dtype) * jax.nn.gelu(u_g)
    o = jnp.concatenate([y_f, y_l], axis=-1) @ w_out
    x = x + g1 * o
    h2 = rmsnorm(x, norm2_g) * (1.0 + sc2) + sh2
    x = x + g2 * peer_ffn(h2, peer_wq, peer_k1, peer_k2, peer_u, peer_v)
    return x


def encode(x, c, layer_params, norm_f_g):
    for l in range(DEPTH):
        x = encoder_layer(x, c, *[p_arr[l] for p_arr in layer_params])
    return rmsnorm(x, norm_f_g)


def setup_inputs(seed: int = 0) -> dict:
    key = jax.random.key(seed)
    ks = jax.random.split(key, 32)
    f32 = jnp.float32
    D = D_MODEL
    nrm = lambda k, shape, s: jax.random.normal(k, shape, f32) * s
    p_a = jax.random.uniform(ks[14], (DEPTH, 2, LRU_WIDTH), f32, minval=0.9, maxval=0.999)
    s_a = p_a ** (1.0 / RGLRU_C)
    lru_lambda = jnp.log(s_a) - jnp.log1p(-s_a)
    return {
        "x_prompt": nrm(ks[0], (BATCH, SEQ, D), 1.0),
        "x_sample": nrm(ks[1], (DEC_BATCH, DEC_SEQ, D), 1.0),
        "c_prompt": nrm(ks[2], (BATCH, D), 1.0),
        "c_sample": nrm(ks[3], (DEC_BATCH, D), 1.0),
        "ada_w": nrm(ks[4], (DEPTH, D, 6 * D), 0.5 * D ** -0.5),
        "ada_b": nrm(ks[5], (DEPTH, 6 * D), 0.01),
        "norm1_g": 1.0 + nrm(ks[6], (DEPTH, D), 0.02),
        "w_in": nrm(ks[7], (DEPTH, D, IN_WIDTH), D ** -0.5),
        "fnet_w": nrm(ks[8], (DEPTH, FNET_GROUPS, FNET_GROUP_DIM, FNET_GROUP_DIM), FNET_GROUP_DIM ** -0.5),
        "conv_w": nrm(ks[9], (DEPTH, CONV_WIDTH, LRU_WIDTH), CONV_WIDTH ** -0.5),
        "conv_b": nrm(ks[10], (DEPTH, LRU_WIDTH), 0.01),
        "lru_wa": nrm(ks[11], (DEPTH, 2, LRU_HEADS, LRU_HEAD_DIM, LRU_HEAD_DIM), LRU_HEAD_DIM ** -0.5),
        "lru_ba": nrm(ks[12], (DEPTH, 2, LRU_HEADS, LRU_HEAD_DIM), 0.01),
        "lru_wx": nrm(ks[13], (DEPTH, 2, LRU_HEADS, LRU_HEAD_DIM, LRU_HEAD_DIM), LRU_HEAD_DIM ** -0.5),
        "lru_bx": nrm(ks[15], (DEPTH, 2, LRU_HEADS, LRU_HEAD_DIM), 0.01),
        "lru_lambda": lru_lambda,
        "w_out": nrm(ks[16], (DEPTH, MIX_WIDTH, D), MIX_WIDTH ** -0.5),
        "norm2_g": 1.0 + nrm(ks[17], (DEPTH, D), 0.02),
        "peer_wq": nrm(ks[18], (DEPTH, D, PEER_HEADS * 2 * PEER_KEY_DIM), D ** -0.5),
        "peer_k1": nrm(ks[19], (DEPTH, PEER_HEADS, PEER_NKEYS, PEER_KEY_DIM), PEER_KEY_DIM ** -0.5),
        "peer_k2": nrm(ks[20], (DEPTH, PEER_HEADS, PEER_NKEYS, PEER_KEY_DIM), PEER_KEY_DIM ** -0.5),
        "peer_u": nrm(ks[21], (DEPTH, PEER_EXPERTS, D), D ** -0.5),
        "peer_v": nrm(ks[22], (DEPTH, PEER_EXPERTS, D), 0.1),
        "norm_f_g": 1.0 + nrm(ks[23], (D,), 0.02),
    }


def reference(x_prompt, x_sample, c_prompt, c_sample, ada_w, ada_b, norm1_g, w_in, fnet_w, conv_w, conv_b, lru_wa, lru_ba, lru_wx, lru_bx, lru_lambda, w_out, norm2_g, peer_wq, peer_k1, peer_k2, peer_u, peer_v, norm_f_g):
    layer_params = (ada_w, ada_b, norm1_g, w_in, fnet_w, conv_w, conv_b, lru_wa, lru_ba, lru_wx, lru_bx, lru_lambda, w_out, norm2_g, peer_wq, peer_k1, peer_k2, peer_u, peer_v)
    y_prompt = encode(x_prompt, c_prompt, layer_params, norm_f_g)
    y_sample = encode(x_sample, c_sample, layer_params, norm_f_g)
    return (y_prompt, y_sample)
```

```python
import functools
import math

import numpy as np
import jax
import jax.numpy as jnp
from jax import lax
from jax.experimental import pallas as pl
from jax.experimental.pallas import tpu as pltpu

F32 = jnp.float32
BF16 = jnp.bfloat16
HIGHEST = lax.Precision.HIGHEST

D_MODEL = 1024
FNET_WIDTH = 512
LRU_WIDTH = 512
FNET_GROUP_DIM = 64
LRU_HEADS = 8
CONV_WIDTH = 4
CONV_LEFT = 2
RGLRU_C = 8.0
IN_WIDTH = FNET_WIDTH + 2 * LRU_WIDTH
PEER_HEADS = 8
PEER_NKEYS = 128
PEER_EXPERTS = PEER_NKEYS * PEER_NKEYS
PEER_KEY_DIM = 128
PEER_TOPK = 16
NORM_EPS = 1e-6

LANES = 128
SUBLANES = 8
DFT_N2 = 128
VMEM_LIMIT = 56 * 1024 * 1024

TOKEN_TILE = 512
SCAN_TILE = 1024
SCAN_SEG_PAD = 4
PEER_TOKEN_TILE = 512
PEER_EXPERT_TILE = 1024
PEER_LANE_CHUNK = 256


def _params(*sem):
    return pltpu.CompilerParams(dimension_semantics=sem, vmem_limit_bytes=VMEM_LIMIT)


def _gelu(x):
    return jax.nn.gelu(x, approximate=True)


def _rms(x, g):
    return x * lax.rsqrt(jnp.mean(x * x, axis=-1, keepdims=True) + NORM_EPS) * g


def _ada_kernel(c_ref, w_ref, b_ref, o_ref):
    c = c_ref[...]
    s = c * jax.nn.sigmoid(c)
    o_ref[...] = jnp.dot(s.astype(BF16), w_ref[...].astype(BF16), preferred_element_type=F32) + b_ref[...]


def _ada(c_pad, ada_w, ada_b):
    n = ada_w.shape[1]
    tn = 1536
    return pl.pallas_call(
        _ada_kernel,
        grid=(n // tn,),
        in_specs=[pl.BlockSpec((SUBLANES, D_MODEL), lambda j: (0, 0)),
                  pl.BlockSpec((D_MODEL, tn), lambda j: (0, j)),
                  pl.BlockSpec((1, tn), lambda j: (0, j))],
        out_specs=pl.BlockSpec((SUBLANES, tn), lambda j: (0, j)),
        out_shape=jax.ShapeDtypeStruct((SUBLANES, n), F32),
        compiler_params=_params("arbitrary"),
        name="adaln",
    )(c_pad, ada_w, ada_b.reshape(1, n))


def _inproj_kernel(x_ref, sh_ref, sc_ref, g_ref, w_ref, cs_ref, ucs_ref, ux_ref, gg_ref):
    h = _rms(x_ref[...], g_ref[...]) * (1.0 + sc_ref[...]) + sh_ref[...]
    p = jnp.dot(h.astype(BF16), w_ref[...], preferred_element_type=F32)
    for j in range(FNET_WIDTH // LANES):
        r = jnp.dot(p[:, j * LANES:(j + 1) * LANES], cs_ref[...], precision=HIGHEST,
                    preferred_element_type=F32)
        ucs_ref[:, j * LANES:(j + 1) * LANES] = r[:, :LANES]
        ucs_ref[:, FNET_WIDTH + j * LANES:FNET_WIDTH + (j + 1) * LANES] = r[:, LANES:]
    ux_ref[...] = p[:, FNET_WIDTH:FNET_WIDTH + LRU_WIDTH]
    gg_ref[...] = _gelu(p[:, FNET_WIDTH + LRU_WIDTH:])


def _channel_dft_matrix():
    c = np.arange(FNET_GROUP_DIM)
    ang = 2.0 * np.pi * np.outer(c, c) / FNET_GROUP_DIM
    scale = 1.0 / math.sqrt(FNET_GROUP_DIM)
    z = np.zeros((FNET_GROUP_DIM, FNET_GROUP_DIM))
    cc, ss = np.cos(ang) * scale, np.sin(ang) * scale
    return np.block([[cc, z, ss, z], [z, cc, z, ss]]).astype(np.float32)


def _inproj(x, mod3, norm1_g, w_in_bf, b0):
    B, S, _ = x.shape
    tm = TOKEN_TILE
    row = lambda k: pl.BlockSpec((None, 1, D_MODEL), lambda b, i, k=k: (b + b0, 0, k))
    tok = lambda w: pl.BlockSpec((None, tm, w), lambda b, i: (b, i, 0))
    full = lambda a: pl.BlockSpec(a.shape, lambda b, i: (0,) * a.ndim)
    cs = jnp.asarray(_channel_dft_matrix())
    return pl.pallas_call(
        _inproj_kernel,
        grid=(B, S // tm),
        in_specs=[tok(D_MODEL), row(0), row(1), full(norm1_g), full(w_in_bf), full(cs)],
        out_specs=[tok(2 * FNET_WIDTH), tok(LRU_WIDTH), tok(LRU_WIDTH)],
        out_shape=[jax.ShapeDtypeStruct((B, S, 2 * FNET_WIDTH), F32),
                   jax.ShapeDtypeStruct((B, S, LRU_WIDTH), F32),
                   jax.ShapeDtypeStruct((B, S, LRU_WIDTH), F32)],
        compiler_params=_params("parallel", "parallel"),
        name="inproj",
    )(x, mod3, mod3, norm1_g, w_in_bf, cs)


def _dft1_kernel(x_ref, fa_ref, fb_ref, o_ref, *, nb, n1):
    for j in range(nb):
        xc = x_ref[:, j * 2 * FNET_WIDTH:j * 2 * FNET_WIDTH + FNET_WIDTH]
        xs = x_ref[:, j * 2 * FNET_WIDTH + FNET_WIDTH:(j + 1) * 2 * FNET_WIDTH]
        pq = (jnp.dot(fa_ref[...], xc, precision=HIGHEST, preferred_element_type=F32)
              + jnp.dot(fb_ref[...], xs, precision=HIGHEST, preferred_element_type=F32))
        o_ref[0, :, j * FNET_WIDTH:(j + 1) * FNET_WIDTH] = pq[:n1]
        o_ref[1, :, j * FNET_WIDTH:(j + 1) * FNET_WIDTH] = pq[n1:]


def _dft2_kernel(pq_ref, o_ref, *, n1, n):
    k1 = pl.program_id(1)
    k2 = lax.broadcasted_iota(jnp.int32, (DFT_N2, DFT_N2), 0)
    n2 = lax.broadcasted_iota(jnp.int32, (DFT_N2, DFT_N2), 1)
    m = (n2 * (k1 + n1 * k2)) & (n - 1)
    ang = m.astype(F32) * (2.0 * math.pi / n)
    scale = 1.0 / math.sqrt(n)
    out = (jnp.dot(jnp.cos(ang) * scale, pq_ref[0], precision=HIGHEST, preferred_element_type=F32)
           + jnp.dot(jnp.sin(ang) * scale, pq_ref[1], precision=HIGHEST, preferred_element_type=F32))
    o_ref[...] = out.astype(o_ref.dtype)


def _seq_dft(ucs):
    B, S, _ = ucs.shape
    n1 = S // DFT_N2
    assert n1 * DFT_N2 == S and (S & (S - 1)) == 0 and n1 % SUBLANES == 0
    k = np.arange(n1)
    ang = 2.0 * np.pi * np.outer(k, k) / n1
    c1, s1 = np.cos(ang), np.sin(ang)
    fa = jnp.asarray(np.concatenate([c1, -s1], axis=0).astype(np.float32))
    fb = jnp.asarray(np.concatenate([-s1, -c1], axis=0).astype(np.float32))
    nb = 4
    x1 = ucs.reshape(B, n1, DFT_N2 * 2 * FNET_WIDTH)
    pq = pl.pallas_call(
        functools.partial(_dft1_kernel, nb=nb, n1=n1),
        grid=(B, DFT_N2 // nb),
        in_specs=[pl.BlockSpec((None, n1, nb * 2 * FNET_WIDTH), lambda b, j: (b, 0, j)),
                  pl.BlockSpec(fa.shape, lambda b, j: (0, 0)),
                  pl.BlockSpec(fb.shape, lambda b, j: (0, 0))],
        out_specs=pl.BlockSpec((None, 2, n1, nb * FNET_WIDTH), lambda b, j: (b, 0, 0, j)),
        out_shape=jax.ShapeDtypeStruct((B, 2, n1, DFT_N2 * FNET_WIDTH), F32),
        compiler_params=_params("parallel", "parallel"),
        name="seq_dft_stage1",
    )(x1, fa, fb)
    pq5 = pq.reshape(B, 2, n1, DFT_N2, FNET_WIDTH)
    f3 = pl.pallas_call(
        functools.partial(_dft2_kernel, n1=n1, n=S),
        grid=(B, n1),
        in_specs=[pl.BlockSpec((None, 2, None, DFT_N2, FNET_WIDTH), lambda b, k: (b, 0, k, 0, 0))],
        out_specs=pl.BlockSpec((None, None, DFT_N2, FNET_WIDTH), lambda b, k: (b, k, 0, 0)),
        out_shape=jax.ShapeDtypeStruct((B, n1, DFT_N2, FNET_WIDTH), BF16),
        compiler_params=_params("parallel", "parallel"),
        name="seq_dft_stage2",
    )(pq5)
    return jnp.swapaxes(f3, 1, 2).reshape(B, S, FNET_WIDTH)


def _scan_kernel(*refs, reverse, tile, nt):
    if reverse:
        (u_ref, prev_ref, next_ref, cw_ref, cb_ref, wa_ref, ba_ref, wx_ref, bx_ref, lam_ref,
         hf_ref, gg_ref, o_ref, ext_s, a_s, b_s, h_s, p_s, carry_s) = refs
    else:
        (u_ref, prev_ref, next_ref, cw_ref, cb_ref, wa_ref, ba_ref, wx_ref, bx_ref, lam_ref,
         o_ref, ext_s, a_s, b_s, h_s, p_s, carry_s) = refs
    i = pl.program_id(1)
    ti = nt - 1 - i if reverse else i
    seg = tile // SUBLANES
    pitch = seg + SCAN_SEG_PAD
    nslab = LRU_WIDTH // LANES

    @pl.when(i == 0)
    def _():
        carry_s[...] = jnp.zeros_like(carry_s)

    ext_s[0:SUBLANES, :] = jnp.where(ti == 0, 0.0, prev_ref[...])
    ext_s[SUBLANES:SUBLANES + tile, :] = u_ref[...]
    ext_s[SUBLANES + tile:, :] = jnp.where(ti == nt - 1, 0.0, next_ref[...])
    xr = cb_ref[...] + ext_s[pl.ds(SUBLANES - CONV_LEFT, tile), :] * cw_ref[0:1, :]
    for k in range(1, CONV_WIDTH):
        xr = xr + ext_s[pl.ds(SUBLANES - CONV_LEFT + k, tile), :] * cw_ref[k:k + 1, :]

    xb = xr.astype(BF16)
    r = jax.nn.sigmoid(jnp.dot(xb, wa_ref[...], preferred_element_type=F32) + ba_ref[...])
    g = jax.nn.sigmoid(jnp.dot(xb, wx_ref[...], preferred_element_type=F32) + bx_ref[...])
    lam = lam_ref[...]
    softplus_neg = jnp.maximum(-lam, 0.0) + jnp.log1p(jnp.exp(-jnp.abs(lam)))
    log_a = (-RGLRU_C * softplus_neg) * r
    a = jnp.exp(log_a)
    bvals = jnp.sqrt(-jnp.tanh(log_a) * (1.0 + a * a)) * g * xr

    for c in range(nslab):
        for j in range(SUBLANES):
            a_s[c, j * pitch:j * pitch + seg, :] = a[j * seg:(j + 1) * seg, c * LANES:(c + 1) * LANES]
            b_s[c, j * pitch:j * pitch + seg, :] = bvals[j * seg:(j + 1) * seg, c * LANES:(c + 1) * LANES]

    def body(step, carry):
        rr = seg - 1 - step if reverse else step
        out = []
        for c in range(nslab):
            h, p = carry[2 * c], carry[2 * c + 1]
            av = a_s[c, pl.ds(rr, SUBLANES, stride=pitch), :]
            bv = b_s[c, pl.ds(rr, SUBLANES, stride=pitch), :]
            h = av * h + bv
            p = av * p
            h_s[c, pl.ds(rr, SUBLANES, stride=pitch), :] = h
            p_s[c, pl.ds(rr, SUBLANES, stride=pitch), :] = p
            out += [h, p]
        return tuple(out)

    init = []
    for c in range(nslab):
        init += [jnp.zeros((SUBLANES, LANES), F32), jnp.ones((SUBLANES, LANES), F32)]
    fin = lax.fori_loop(0, seg, body, tuple(init), unroll=4)

    for c in range(nslab):
        hl, pf = fin[2 * c], fin[2 * c + 1]
        state = carry_s[:, c * LANES:(c + 1) * LANES]
        order = range(SUBLANES - 1, -1, -1) if reverse else range(SUBLANES)
        for j in order:
            hloc = h_s[c, j * pitch:j * pitch + seg, :]
            ploc = p_s[c, j * pitch:j * pitch + seg, :]
            hrow = hloc + ploc * state
            rows = slice(j * seg, (j + 1) * seg)
            cols = slice(c * LANES, (c + 1) * LANES)
            if reverse:
                o_ref[rows, cols] = ((hf_ref[rows, cols] + hrow) * gg_ref[rows, cols]).astype(o_ref.dtype)
            else:
                o_ref[rows, cols] = hrow
            state = hl[j:j + 1, :] + pf[j:j + 1, :] * state
        carry_s[:, c * LANES:(c + 1) * LANES] = state


def _scan(ux, conv_w, conv_b, wa, ba, wx, bx, lam, reverse, hf=None, gg=None):
    B, S, W = ux.shape
    tile = min(SCAN_TILE, S)
    nt = S // tile
    nh = tile // SUBLANES
    last = S // SUBLANES - 1
    t_of = (lambda i: nt - 1 - i) if reverse else (lambda i: i)
    tok = pl.BlockSpec((None, tile, W), lambda b, i: (b, t_of(i), 0))
    prev = pl.BlockSpec((None, SUBLANES, W), lambda b, i: (b, jnp.maximum(t_of(i) * nh - 1, 0), 0))
    nxt = pl.BlockSpec((None, SUBLANES, W), lambda b, i: (b, jnp.minimum((t_of(i) + 1) * nh, last), 0))
    full = lambda a: pl.BlockSpec(a.shape, lambda b, i: (0,) * a.ndim)
    args = [ux, ux, ux, conv_w, conv_b, wa, ba, wx, bx, lam]
    specs = [tok, prev, nxt] + [full(a) for a in args[3:]]
    if reverse:
        args += [hf, gg]
        specs += [tok, tok]
    pitch = tile // SUBLANES + SCAN_SEG_PAD
    slab = pltpu.VMEM((W // LANES, SUBLANES * pitch, LANES), F32)
    return pl.pallas_call(
        functools.partial(_scan_kernel, reverse=reverse, tile=tile, nt=nt),
        grid=(B, nt),
        in_specs=specs,
        out_specs=tok,
        out_shape=jax.ShapeDtypeStruct((B, S, W), BF16 if reverse else F32),
        scratch_shapes=[pltpu.VMEM((tile + 2 * SUBLANES, W), F32), slab, slab, slab, slab,
                        pltpu.VMEM((1, W), F32)],
        compiler_params=_params("parallel", "arbitrary"),
        name="rglru_bwd" if reverse else "rglru_fwd",
    )(*args)


def _outproj_kernel(f_ref, yl_ref, x_ref, g1_ref, sh_ref, sc_ref, n2_ref, fw_ref, wo_ref, x1_ref, h2_ref):
    yf = jnp.dot(f_ref[...], fw_ref[...], preferred_element_type=F32)
    o = (jnp.dot(yf.astype(BF16), wo_ref[0:FNET_WIDTH, :], preferred_element_type=F32)
         + jnp.dot(yl_ref[...], wo_ref[FNET_WIDTH:, :], preferred_element_type=F32))
    x1 = x_ref[...] + g1_ref[...] * o
    x1_ref[...] = x1
    h2_ref[...] = (_rms(x1, n2_ref[...]) * (1.0 + sc_ref[...]) + sh_ref[...]).astype(h2_ref.dtype)


def _outproj(f, yl, x, mod3, norm2_g, fw_bd, w_out_bf, b0):
    B, S, _ = x.shape
    tm = TOKEN_TILE
    row = lambda k: pl.BlockSpec((None, 1, D_MODEL), lambda b, i, k=k: (b + b0, 0, k))
    tok = lambda w: pl.BlockSpec((None, tm, w), lambda b, i: (b, i, 0))
    full = lambda a: pl.BlockSpec(a.shape, lambda b, i: (0,) * a.ndim)
    return pl.pallas_call(
        _outproj_kernel,
        grid=(B, S // tm),
        in_specs=[tok(FNET_WIDTH), tok(LRU_WIDTH), tok(D_MODEL), row(2), row(3), row(4),
                  full(norm2_g), full(fw_bd), full(w_out_bf)],
        out_specs=[tok(D_MODEL), tok(D_MODEL)],
        out_shape=[jax.ShapeDtypeStruct((B, S, D_MODEL), F32),
                   jax.ShapeDtypeStruct((B, S, D_MODEL), BF16)],
        compiler_params=_params("parallel", "parallel"),
        name="outproj",
    )(f, yl, x, mod3, mod3, mod3, norm2_g, fw_bd, w_out_bf)


def _peer_candidates():
    return [(a, b) for a in range(PEER_TOPK) for b in range(PEER_TOPK) if (a + 1) * (b + 1) <= PEER_TOPK]


def _route_kernel(h2_ref, wq_ref, k1_ref, k2_ref, s1_ref, s2_ref, e1_ref, e2_ref, tau_ref):
    q = jnp.dot(h2_ref[...], wq_ref[...], preferred_element_type=F32).astype(BF16)
    half = PEER_HEADS * PEER_KEY_DIM
    nt_dims = (((1,), (1,)), ((), ()))
    s1_ref[...] = lax.dot_general(k1_ref[...], q[:, :half], nt_dims, preferred_element_type=F32)
    s2_ref[...] = lax.dot_general(k2_ref[...], q[:, half:], nt_dims, preferred_element_type=F32)
    tm = h2_ref.shape[0]

    def top16(s_ref):
        per_head = []
        for h in range(PEER_HEADS):
            sh = s_ref[h * PEER_NKEYS:(h + 1) * PEER_NKEYS, :]
            prev = jnp.full((1, tm), jnp.inf, F32)
            vals = []
            for _ in range(PEER_TOPK):
                prev = jnp.max(jnp.where(sh < prev, sh, -jnp.inf), axis=0, keepdims=True)
                vals.append(prev)
            per_head.append(vals)
        return [jnp.concatenate([per_head[h][k] for h in range(PEER_HEADS)], axis=0) for k in range(PEER_TOPK)]

    t1 = top16(s1_ref)
    t2 = top16(s2_ref)
    cands = [t1[a] + t2[b] for a, b in _peer_candidates()]
    prev = jnp.full((PEER_HEADS, tm), jnp.inf, F32)
    tops = []
    for _ in range(PEER_TOPK):
        m = jnp.full((PEER_HEADS, tm), -jnp.inf, F32)
        for c in cands:
            m = jnp.maximum(m, jnp.where(c < prev, c, -jnp.inf))
        tops.append(m)
        prev = m
    tau_ref[...] = tops[PEER_TOPK - 1]
    z = jnp.ones((PEER_HEADS, tm), F32)
    for k in range(1, PEER_TOPK):
        z = z + jnp.exp(tops[k] - tops[0])
    zinv = 1.0 / z
    for h in range(PEER_HEADS):
        rows = slice(h * PEER_NKEYS, (h + 1) * PEER_NKEYS)
        e1_ref[rows, :] = jnp.exp(s1_ref[rows, :] - t1[0][h:h + 1, :])
        e2_ref[rows, :] = jnp.exp(s2_ref[rows, :] - t2[0][h:h + 1, :]) * zinv[h:h + 1, :]


def _route(h2, wq_bf, k1_bd, k2_bd):
    B, S, _ = h2.shape
    tm = PEER_TOKEN_TILE
    hk = PEER_HEADS * PEER_NKEYS
    full = lambda a: pl.BlockSpec(a.shape, lambda b, i: (0,) * a.ndim)
    tr = lambda r: pl.BlockSpec((None, r, tm), lambda b, i: (b, 0, i))
    return pl.pallas_call(
        _route_kernel,
        grid=(B, S // tm),
        in_specs=[pl.BlockSpec((None, tm, D_MODEL), lambda b, i: (b, i, 0)),
                  full(wq_bf), full(k1_bd), full(k2_bd)],
        out_specs=[tr(hk), tr(hk), tr(hk), tr(hk), tr(PEER_HEADS)],
        out_shape=[jax.ShapeDtypeStruct((B, hk, S), F32)] * 4
        + [jax.ShapeDtypeStruct((B, PEER_HEADS, S), F32)],
        compiler_params=_params("parallel", "parallel"),
        name="peer_route",
    )(h2, wq_bf, k1_bd, k2_bd)


def _peer_kernel(h2_ref, u_ref, vt_ref, s1_ref, s2_ref, e1_ref, e2_ref, tau_ref, x1_ref, g2_ref, nf_ref,
                 o_ref, acc_s, w_s, *, nj):
    j = pl.program_id(2)
    te, tm = u_ref.shape[0], h2_ref.shape[0]

    @pl.when(j == 0)
    def _():
        acc_s[...] = jnp.zeros_like(acc_s)

    act = _gelu(lax.dot_general(u_ref[...], h2_ref[...], (((1,), (1,)), ((), ())),
                                preferred_element_type=F32))
    for ii in range(te // PEER_NKEYS):
        i1 = j * (te // PEER_NKEYS) + ii
        for t0 in range(0, tm, PEER_LANE_CHUNK):
            cols = slice(t0, t0 + PEER_LANE_CHUNK)
            gate = jnp.zeros((PEER_NKEYS, PEER_LANE_CHUNK), F32)
            for h in range(PEER_HEADS):
                rows = slice(h * PEER_NKEYS, (h + 1) * PEER_NKEYS)
                s1row = s1_ref[pl.ds(h * PEER_NKEYS + i1, 1), cols]
                e1row = e1_ref[pl.ds(h * PEER_NKEYS + i1, 1), cols]
                sel = (s1row + s2_ref[rows, cols]) >= tau_ref[h:h + 1, cols]
                gate = gate + jnp.where(sel, e1row * e2_ref[rows, cols], 0.0)
            w_s[ii * PEER_NKEYS:(ii + 1) * PEER_NKEYS, cols] = (
                gate * act[ii * PEER_NKEYS:(ii + 1) * PEER_NKEYS, cols]).astype(BF16)
    acc_s[...] += jnp.dot(vt_ref[...], w_s[...], preferred_element_type=F32)

    @pl.when(j == nj - 1)
    def _():
        x2 = x1_ref[...] + g2_ref[...] * acc_s[...].T
        o_ref[...] = _rms(x2, nf_ref[...])


def _peer(h2, u_bf, vt_bf, s1, s2, e1, e2, tau, x1, mod3, norm_f_g, b0):
    B, S, _ = h2.shape
    tm, te = PEER_TOKEN_TILE, PEER_EXPERT_TILE
    nj = PEER_EXPERTS // te
    hk = PEER_HEADS * PEER_NKEYS
    tok = lambda w: pl.BlockSpec((None, tm, w), lambda b, i, j: (b, i, 0))
    tr = lambda r: pl.BlockSpec((None, r, tm), lambda b, i, j: (b, 0, i))
    return pl.pallas_call(
        functools.partial(_peer_kernel, nj=nj),
        grid=(B, S // tm, nj),
        in_specs=[tok(D_MODEL),
                  pl.BlockSpec((te, D_MODEL), lambda b, i, j: (j, 0)),
                  pl.BlockSpec((D_MODEL, te), lambda b, i, j: (0, j)),
                  tr(hk), tr(hk), tr(hk), tr(hk), tr(PEER_HEADS), tok(D_MODEL),
                  pl.BlockSpec((None, 1, D_MODEL), lambda b, i, j: (b + b0, 0, 5)),
                  pl.BlockSpec((1, D_MODEL), lambda b, i, j: (0, 0))],
        out_specs=tok(D_MODEL),
        out_shape=jax.ShapeDtypeStruct((B, S, D_MODEL), F32),
        scratch_shapes=[pltpu.VMEM((D_MODEL, tm), F32), pltpu.VMEM((te, tm), BF16)],
        compiler_params=_params("parallel", "parallel", "arbitrary"),
        name="peer_dense",
    )(h2, u_bf, vt_bf, s1, s2, e1, e2, tau, x1, mod3, norm_f_g)


def _block_diag(w):
    h, a, b = w.shape
    eye = jnp.eye(h, dtype=w.dtype)
    return (w[:, :, None, :] * eye[:, None, :, None]).reshape(h * a, h * b)


def _encode(x, mod3, b0, p):
    ucs, ux, gg = _inproj(x, mod3, p["norm1_g"], p["w_in"], b0)
    f = _seq_dft(ucs)
    hf = _scan(ux, p["conv_w"], p["conv_b"], p["wa"][0], p["ba"][0], p["wx"][0], p["bx"][0], p["lam"][0],
               reverse=False)
    yl = _scan(ux, p["conv_w"], p["conv_b"], p["wa"][1], p["ba"][1], p["wx"][1], p["bx"][1], p["lam"][1],
               reverse=True, hf=hf, gg=gg)
    x1, h2 = _outproj(f, yl, x, mod3, p["norm2_g"], p["fnet_w"], p["w_out"], b0)
    s1, s2, e1, e2, tau = _route(h2, p["wq"], p["k1"], p["k2"])
    return _peer(h2, p["u"], p["vt"], s1, s2, e1, e2, tau, x1, mod3, p["norm_f_g"], b0)


def kernel(x_prompt, x_sample, c_prompt, c_sample, ada_w, ada_b, norm1_g, w_in, fnet_w, conv_w, conv_b, lru_wa, lru_ba, lru_wx, lru_bx, lru_lambda, w_out, norm2_g, peer_wq, peer_k1, peer_k2, peer_u, peer_v, norm_f_g):
    nb_p, nb_s = x_prompt.shape[0], x_sample.shape[0]
    assert nb_p + nb_s <= SUBLANES
    c_all = jnp.concatenate([c_prompt, c_sample], axis=0)
    c_pad = jnp.pad(c_all, ((0, SUBLANES - c_all.shape[0]), (0, 0)))
    mod = _ada(c_pad, ada_w[0], ada_b[0])
    mod3 = mod.reshape(SUBLANES, 1, 6 * D_MODEL)

    hd = PEER_HEADS * PEER_KEY_DIM
    wq = peer_wq[0].reshape(D_MODEL, PEER_HEADS, 2, PEER_KEY_DIM).transpose(0, 2, 1, 3).reshape(D_MODEL, 2 * hd)
    p = dict(
        norm1_g=norm1_g[0].reshape(1, D_MODEL),
        w_in=w_in[0].astype(BF16),
        fnet_w=_block_diag(fnet_w[0]).astype(BF16),
        conv_w=conv_w[0], conv_b=conv_b[0].reshape(1, LRU_WIDTH),
        wa=[_block_diag(lru_wa[0, d]).astype(BF16) for d in range(2)],
        wx=[_block_diag(lru_wx[0, d]).astype(BF16) for d in range(2)],
        ba=[lru_ba[0, d].reshape(1, LRU_WIDTH) for d in range(2)],
        bx=[lru_bx[0, d].reshape(1, LRU_WIDTH) for d in range(2)],
        lam=[lru_lambda[0, d].reshape(1, LRU_WIDTH) for d in range(2)],
        w_out=w_out[0].astype(BF16),
        norm2_g=norm2_g[0].reshape(1, D_MODEL),
        wq=wq.astype(BF16),
        k1=_block_diag(peer_k1[0]).astype(BF16),
        k2=_block_diag(peer_k2[0]).astype(BF16),
        u=peer_u[0].astype(BF16),
        vt=peer_v[0].T.astype(BF16),
        norm_f_g=norm_f_g.reshape(1, D_MODEL),
    )
    y_prompt = _encode(x_prompt, mod3, 0, p)
    y_sample = _encode(x_sample, mod3, nb_p, p)
    return (y_prompt, y_sample)
```

```python
import functools
import math

import numpy as np
import jax
import jax.numpy as jnp
from jax import lax
from jax.experimental import pallas as pl
from jax.experimental.pallas import tpu as pltpu

F32 = jnp.float32
BF16 = jnp.bfloat16
HIGHEST = lax.Precision.HIGHEST

D_MODEL = 1024
FNET_WIDTH = 512
LRU_WIDTH = 512
FNET_GROUP_DIM = 64
LRU_HEADS = 8
CONV_WIDTH = 4
CONV_LEFT = 2
RGLRU_C = 8.0
IN_WIDTH = FNET_WIDTH + 2 * LRU_WIDTH
PEER_HEADS = 8
PEER_NKEYS = 128
PEER_EXPERTS = PEER_NKEYS * PEER_NKEYS
PEER_KEY_DIM = 128
PEER_TOPK = 16
NORM_EPS = 1e-6

LANES = 128
SUBLANES = 8
DFT_N2 = 128
VMEM_LIMIT = 56 * 1024 * 1024

TOKEN_TILE = 512
SCAN_TILE = 1024
SCAN_SEG_PAD = 4
PEER_TOKEN_TILE = 512
PEER_EXPERT_TILE = 2048
PEER_LANE_CHUNK = 256


def _params(*sem):
    return pltpu.CompilerParams(dimension_semantics=sem, vmem_limit_bytes=VMEM_LIMIT)


def _gelu(x):
    return jax.nn.gelu(x, approximate=True)


def _rms(x, g):
    return x * lax.rsqrt(jnp.mean(x * x, axis=-1, keepdims=True) + NORM_EPS) * g


def _ada_kernel(c_ref, w_ref, b_ref, o_ref):
    c = c_ref[...]
    s = c * jax.nn.sigmoid(c)
    o_ref[...] = jnp.dot(s.astype(BF16), w_ref[...].astype(BF16), preferred_element_type=F32) + b_ref[...]


def _ada(c_pad, ada_w, ada_b):
    n = ada_w.shape[1]
    tn = 1536
    return pl.pallas_call(
        _ada_kernel,
        grid=(n // tn,),
        in_specs=[pl.BlockSpec((SUBLANES, D_MODEL), lambda j: (0, 0)),
                  pl.BlockSpec((D_MODEL, tn), lambda j: (0, j)),
                  pl.BlockSpec((1, tn), lambda j: (0, j))],
        out_specs=pl.BlockSpec((SUBLANES, tn), lambda j: (0, j)),
        out_shape=jax.ShapeDtypeStruct((SUBLANES, n), F32),
        compiler_params=_params("arbitrary"),
        name="adaln",
    )(c_pad, ada_w, ada_b.reshape(1, n))


def _inproj_kernel(x_ref, sh_ref, sc_ref, g_ref, w_ref, cs_ref, ucs_ref, ux_ref, gg_ref):
    h = _rms(x_ref[...], g_ref[...]) * (1.0 + sc_ref[...]) + sh_ref[...]
    p = jnp.dot(h.astype(BF16), w_ref[...], preferred_element_type=F32)
    for j in range(FNET_WIDTH // LANES):
        r = jnp.dot(p[:, j * LANES:(j + 1) * LANES], cs_ref[...], precision=HIGHEST,
                    preferred_element_type=F32)
        ucs_ref[:, j * LANES:(j + 1) * LANES] = r[:, :LANES]
        ucs_ref[:, FNET_WIDTH + j * LANES:FNET_WIDTH + (j + 1) * LANES] = r[:, LANES:]
    ux_ref[...] = p[:, FNET_WIDTH:FNET_WIDTH + LRU_WIDTH]
    gg_ref[...] = _gelu(p[:, FNET_WIDTH + LRU_WIDTH:])


def _channel_dft_matrix():
    c = np.arange(FNET_GROUP_DIM)
    ang = 2.0 * np.pi * np.outer(c, c) / FNET_GROUP_DIM
    scale = 1.0 / math.sqrt(FNET_GROUP_DIM)
    z = np.zeros((FNET_GROUP_DIM, FNET_GROUP_DIM))
    cc, ss = np.cos(ang) * scale, np.sin(ang) * scale
    return np.block([[cc, z, ss, z], [z, cc, z, ss]]).astype(np.float32)


def _inproj(x, mod3, norm1_g, w_in_bf, b0):
    B, S, _ = x.shape
    tm = TOKEN_TILE
    row = lambda k: pl.BlockSpec((None, 1, D_MODEL), lambda b, i, k=k: (b + b0, 0, k))
    tok = lambda w: pl.BlockSpec((None, tm, w), lambda b, i: (b, i, 0))
    full = lambda a: pl.BlockSpec(a.shape, lambda b, i: (0,) * a.ndim)
    cs = jnp.asarray(_channel_dft_matrix())
    return pl.pallas_call(
        _inproj_kernel,
        grid=(B, S // tm),
        in_specs=[tok(D_MODEL), row(0), row(1), full(norm1_g), full(w_in_bf), full(cs)],
        out_specs=[tok(2 * FNET_WIDTH), tok(LRU_WIDTH), tok(LRU_WIDTH)],
        out_shape=[jax.ShapeDtypeStruct((B, S, 2 * FNET_WIDTH), F32),
                   jax.ShapeDtypeStruct((B, S, LRU_WIDTH), F32),
                   jax.ShapeDtypeStruct((B, S, LRU_WIDTH), F32)],
        compiler_params=_params("parallel", "parallel"),
        name="inproj",
    )(x, mod3, mod3, norm1_g, w_in_bf, cs)


def _dft1_kernel(x_ref, fa_ref, fb_ref, o_ref, *, nb, n1):
    for j in range(nb):
        xc = x_ref[:, j * 2 * FNET_WIDTH:j * 2 * FNET_WIDTH + FNET_WIDTH]
        xs = x_ref[:, j * 2 * FNET_WIDTH + FNET_WIDTH:(j + 1) * 2 * FNET_WIDTH]
        pq = (jnp.dot(fa_ref[...], xc, precision=HIGHEST, preferred_element_type=F32)
              + jnp.dot(fb_ref[...], xs, precision=HIGHEST, preferred_element_type=F32))
        o_ref[0, :, j * FNET_WIDTH:(j + 1) * FNET_WIDTH] = pq[:n1]
        o_ref[1, :, j * FNET_WIDTH:(j + 1) * FNET_WIDTH] = pq[n1:]


def _dft2_kernel(pq_ref, o_ref, *, n1, n):
    k1 = pl.program_id(1)
    k2 = lax.broadcasted_iota(jnp.int32, (DFT_N2, DFT_N2), 0)
    n2 = lax.broadcasted_iota(jnp.int32, (DFT_N2, DFT_N2), 1)
    m = (n2 * (k1 + n1 * k2)) & (n - 1)
    ang = m.astype(F32) * (2.0 * math.pi / n)
    scale = 1.0 / math.sqrt(n)
    out = (jnp.dot(jnp.cos(ang) * scale, pq_ref[0], precision=HIGHEST, preferred_element_type=F32)
           + jnp.dot(jnp.sin(ang) * scale, pq_ref[1], precision=HIGHEST, preferred_element_type=F32))
    o_ref[...] = out.astype(o_ref.dtype)


def _seq_dft(ucs):
    B, S, _ = ucs.shape
    n1 = S // DFT_N2
    assert n1 * DFT_N2 == S and (S & (S - 1)) == 0 and n1 % SUBLANES == 0
    k = np.arange(n1)
    ang = 2.0 * np.pi * np.outer(k, k) / n1
    c1, s1 = np.cos(ang), np.sin(ang)
    fa = jnp.asarray(np.concatenate([c1, -s1], axis=0).astype(np.float32))
    fb = jnp.asarray(np.concatenate([-s1, -c1], axis=0).astype(np.float32))
    nb = 4
    x1 = ucs.reshape(B, n1, DFT_N2 * 2 * FNET_WIDTH)
    pq = pl.pallas_call(
        functools.partial(_dft1_kernel, nb=nb, n1=n1),
        grid=(B, DFT_N2 // nb),
        in_specs=[pl.BlockSpec((None, n1, nb * 2 * FNET_WIDTH), lambda b, j: (b, 0, j)),
                  pl.BlockSpec(fa.shape, lambda b, j: (0, 0)),
                  pl.BlockSpec(fb.shape, lambda b, j: (0, 0))],
        out_specs=pl.BlockSpec((None, 2, n1, nb * FNET_WIDTH), lambda b, j: (b, 0, 0, j)),
        out_shape=jax.ShapeDtypeStruct((B, 2, n1, DFT_N2 * FNET_WIDTH), F32),
        compiler_params=_params("parallel", "parallel"),
        name="seq_dft_stage1",
    )(x1, fa, fb)
    pq5 = pq.reshape(B, 2, n1, DFT_N2, FNET_WIDTH)
    f3 = pl.pallas_call(
        functools.partial(_dft2_kernel, n1=n1, n=S),
        grid=(B, n1),
        in_specs=[pl.BlockSpec((None, 2, None, DFT_N2, FNET_WIDTH), lambda b, k: (b, 0, k, 0, 0))],
        out_specs=pl.BlockSpec((None, None, DFT_N2, FNET_WIDTH), lambda b, k: (b, k, 0, 0)),
        out_shape=jax.ShapeDtypeStruct((B, n1, DFT_N2, FNET_WIDTH), BF16),
        compiler_params=_params("parallel", "parallel"),
        name="seq_dft_stage2",
    )(pq5)
    return jnp.swapaxes(f3, 1, 2).reshape(B, S, FNET_WIDTH)


def _scan_kernel(*refs, reverse, tile, nt):
    if reverse:
        (u_ref, prev_ref, next_ref, cw_ref, cb_ref, wa_ref, ba_ref, wx_ref, bx_ref, lam_ref,
         hf_ref, gg_ref, o_ref, ext_s, a_s, b_s, h_s, p_s, carry_s) = refs
    else:
        (u_ref, prev_ref, next_ref, cw_ref, cb_ref, wa_ref, ba_ref, wx_ref, bx_ref, lam_ref,
         o_ref, ext_s, a_s, b_s, h_s, p_s, carry_s) = refs
    i = pl.program_id(1)
    ti = nt - 1 - i if reverse else i
    seg = tile // SUBLANES
    pitch = seg + SCAN_SEG_PAD
    nslab = LRU_WIDTH // LANES

    @pl.when(i == 0)
    def _():
        carry_s[...] = jnp.zeros_like(carry_s)

    ext_s[0:SUBLANES, :] = jnp.where(ti == 0, 0.0, prev_ref[...])
    ext_s[SUBLANES:SUBLANES + tile, :] = u_ref[...]
    ext_s[SUBLANES + tile:, :] = jnp.where(ti == nt - 1, 0.0, next_ref[...])
    xr = cb_ref[...] + ext_s[pl.ds(SUBLANES - CONV_LEFT, tile), :] * cw_ref[0:1, :]
    for k in range(1, CONV_WIDTH):
        xr = xr + ext_s[pl.ds(SUBLANES - CONV_LEFT + k, tile), :] * cw_ref[k:k + 1, :]

    xb = xr.astype(BF16)
    r = jax.nn.sigmoid(jnp.dot(xb, wa_ref[...], preferred_element_type=F32) + ba_ref[...])
    g = jax.nn.sigmoid(jnp.dot(xb, wx_ref[...], preferred_element_type=F32) + bx_ref[...])
    lam = lam_ref[...]
    softplus_neg = jnp.maximum(-lam, 0.0) + jnp.log1p(jnp.exp(-jnp.abs(lam)))
    log_a = (-RGLRU_C * softplus_neg) * r
    a = jnp.exp(log_a)
    bvals = jnp.sqrt(-jnp.tanh(log_a) * (1.0 + a * a)) * g * xr

    for c in range(nslab):
        for j in range(SUBLANES):
            a_s[c, j * pitch:j * pitch + seg, :] = a[j * seg:(j + 1) * seg, c * LANES:(c + 1) * LANES]
            b_s[c, j * pitch:j * pitch + seg, :] = bvals[j * seg:(j + 1) * seg, c * LANES:(c + 1) * LANES]

    def body(step, carry):
        rr = seg - 1 - step if reverse else step
        out = []
        for c in range(nslab):
            h, p = carry[2 * c], carry[2 * c + 1]
            av = a_s[c, pl.ds(rr, SUBLANES, stride=pitch), :]
            bv = b_s[c, pl.ds(rr, SUBLANES, stride=pitch), :]
            h = av * h + bv
            p = av * p
            h_s[c, pl.ds(rr, SUBLANES, stride=pitch), :] = h
            p_s[c, pl.ds(rr, SUBLANES, stride=pitch), :] = p
            out += [h, p]
        return tuple(out)

    init = []
    for c in range(nslab):
        init += [jnp.zeros((SUBLANES, LANES), F32), jnp.ones((SUBLANES, LANES), F32)]
    fin = lax.fori_loop(0, seg, body, tuple(init), unroll=4)

    for c in range(nslab):
        hl, pf = fin[2 * c], fin[2 * c + 1]
        state = carry_s[:, c * LANES:(c + 1) * LANES]
        order = range(SUBLANES - 1, -1, -1) if reverse else range(SUBLANES)
        for j in order:
            hloc = h_s[c, j * pitch:j * pitch + seg, :]
            ploc = p_s[c, j * pitch:j * pitch + seg, :]
            hrow = hloc + ploc * state
            rows = slice(j * seg, (j + 1) * seg)
            cols = slice(c * LANES, (c + 1) * LANES)
            if reverse:
                o_ref[rows, cols] = ((hf_ref[rows, cols] + hrow) * gg_ref[rows, cols]).astype(o_ref.dtype)
            else:
                o_ref[rows, cols] = hrow
            state = hl[j:j + 1, :] + pf[j:j + 1, :] * state
        carry_s[:, c * LANES:(c + 1) * LANES] = state


def _scan(ux, conv_w, conv_b, wa, ba, wx, bx, lam, reverse, hf=None, gg=None):
    B, S, W = ux.shape
    tile = min(SCAN_TILE, S)
    nt = S // tile
    nh = tile // SUBLANES
    last = S // SUBLANES - 1
    t_of = (lambda i: nt - 1 - i) if reverse else (lambda i: i)
    tok = pl.BlockSpec((None, tile, W), lambda b, i: (b, t_of(i), 0))
    prev = pl.BlockSpec((None, SUBLANES, W), lambda b, i: (b, jnp.maximum(t_of(i) * nh - 1, 0), 0))
    nxt = pl.BlockSpec((None, SUBLANES, W), lambda b, i: (b, jnp.minimum((t_of(i) + 1) * nh, last), 0))
    full = lambda a: pl.BlockSpec(a.shape, lambda b, i: (0,) * a.ndim)
    args = [ux, ux, ux, conv_w, conv_b, wa, ba, wx, bx, lam]
    specs = [tok, prev, nxt] + [full(a) for a in args[3:]]
    if reverse:
        args += [hf, gg]
        specs += [tok, tok]
    pitch = tile // SUBLANES + SCAN_SEG_PAD
    slab = pltpu.VMEM((W // LANES, SUBLANES * pitch, LANES), F32)
    return pl.pallas_call(
        functools.partial(_scan_kernel, reverse=reverse, tile=tile, nt=nt),
        grid=(B, nt),
        in_specs=specs,
        out_specs=tok,
        out_shape=jax.ShapeDtypeStruct((B, S, W), BF16 if reverse else F32),
        scratch_shapes=[pltpu.VMEM((tile + 2 * SUBLANES, W), F32), slab, slab, slab, slab,
                        pltpu.VMEM((1, W), F32)],
        compiler_params=_params("parallel", "arbitrary"),
        name="rglru_bwd" if reverse else "rglru_fwd",
    )(*args)


def _outproj_kernel(f_ref, yl_ref, x_ref, g1_ref, sh_ref, sc_ref, n2_ref, fw_ref, wo_ref, x1_ref, h2_ref):
    yf = jnp.dot(f_ref[...], fw_ref[...], preferred_element_type=F32)
    o = (jnp.dot(yf.astype(BF16), wo_ref[0:FNET_WIDTH, :], preferred_element_type=F32)
         + jnp.dot(yl_ref[...], wo_ref[FNET_WIDTH:, :], preferred_element_type=F32))
    x1 = x_ref[...] + g1_ref[...] * o
    x1_ref[...] = x1
    h2_ref[...] = (_rms(x1, n2_ref[...]) * (1.0 + sc_ref[...]) + sh_ref[...]).astype(h2_ref.dtype)


def _outproj(f, yl, x, mod3, norm2_g, fw_bd, w_out_bf, b0):
    B, S, _ = x.shape
    tm = TOKEN_TILE
    row = lambda k: pl.BlockSpec((None, 1, D_MODEL), lambda b, i, k=k: (b + b0, 0, k))
    tok = lambda w: pl.BlockSpec((None, tm, w), lambda b, i: (b, i, 0))
    full = lambda a: pl.BlockSpec(a.shape, lambda b, i: (0,) * a.ndim)
    return pl.pallas_call(
        _outproj_kernel,
        grid=(B, S // tm),
        in_specs=[tok(FNET_WIDTH), tok(LRU_WIDTH), tok(D_MODEL), row(2), row(3), row(4),
                  full(norm2_g), full(fw_bd), full(w_out_bf)],
        out_specs=[tok(D_MODEL), tok(D_MODEL)],
        out_shape=[jax.ShapeDtypeStruct((B, S, D_MODEL), F32),
                   jax.ShapeDtypeStruct((B, S, D_MODEL), BF16)],
        compiler_params=_params("parallel", "parallel"),
        name="outproj",
    )(f, yl, x, mod3, mod3, mod3, norm2_g, fw_bd, w_out_bf)


def _dup_bf16(x):
    b = pltpu.bitcast(x, jnp.int32)
    r = lax.shift_right_logical(b + 0x7FFF + (lax.shift_right_logical(b, 16) & 1), 16)
    return r | (r << 16)


def _route_kernel(h2_ref, wq_ref, k1_ref, k2_ref, r2_ref, e2_ref, lim_ref, e1_ref, s1_s, s2_s):
    q = jnp.dot(h2_ref[...], wq_ref[...], preferred_element_type=F32).astype(BF16)
    half = PEER_HEADS * PEER_KEY_DIM
    nt_dims = (((1,), (1,)), ((), ()))
    s1_s[...] = lax.dot_general(k1_ref[...], q[:, :half], nt_dims, preferred_element_type=F32)
    s2_s[...] = lax.dot_general(k2_ref[...], q[:, half:], nt_dims, preferred_element_type=F32)
    tm = h2_ref.shape[0]
    head_rows = lambda h: slice(h * PEER_NKEYS, (h + 1) * PEER_NKEYS)

    def top16(s_ref, rank_ref):
        per_head = []
        for h in range(PEER_HEADS):
            sh = s_ref[head_rows(h), :]
            prev = jnp.max(sh, axis=0, keepdims=True)
            vals = [prev]
            rank = jnp.zeros_like(sh)
            for k in range(1, PEER_TOPK + 1):
                below = sh < prev
                if rank_ref is not None:
                    rank = jnp.where(below, float(k), rank)
                if k < PEER_TOPK:
                    prev = jnp.max(jnp.where(below, sh, -jnp.inf), axis=0, keepdims=True)
                    vals.append(prev)
            if rank_ref is not None:
                rank_ref[head_rows(h), :] = rank.astype(rank_ref.dtype)
            per_head.append(vals)
        return [jnp.concatenate([per_head[h][k] for h in range(PEER_HEADS)], axis=0) for k in range(PEER_TOPK)]

    t1 = top16(s1_s, None)
    t2 = top16(s2_s, r2_ref)
    nb = [PEER_TOPK // (a + 1) for a in range(PEER_TOPK)]
    cands = [[t1[a] + t2[b] for b in range(nb[a])] for a in range(PEER_TOPK)]
    prev = cands[0][0]
    tops = [prev]
    for _ in range(1, PEER_TOPK):
        m = jnp.full((PEER_HEADS, tm), -jnp.inf, F32)
        for row in cands:
            for c in row:
                m = jnp.maximum(m, jnp.where(c < prev, c, -jnp.inf))
        tops.append(m)
        prev = m
    tau = tops[PEER_TOPK - 1]
    z = jnp.ones((PEER_HEADS, tm), F32)
    for k in range(1, PEER_TOPK):
        z = z + jnp.exp(tops[k] - tops[0])
    zinv = 1.0 / z
    big_l = []
    for a in range(PEER_TOPK):
        cnt = jnp.zeros((PEER_HEADS, tm), F32)
        for c in cands[a]:
            cnt = cnt + jnp.where(c >= tau, 1.0, 0.0)
        big_l.append(cnt)
    theta = []
    for j in range(1, PEER_TOPK + 1):
        th = jnp.full((PEER_HEADS, tm), jnp.inf, F32)
        for a in range(PEER_TOPK):
            if nb[a] >= j:
                th = jnp.minimum(th, jnp.where(big_l[a] >= float(j), t1[a], jnp.inf))
        theta.append(th)
    for h in range(PEER_HEADS):
        rows = head_rows(h)
        s1h = s1_s[rows, :]
        lim = jnp.zeros_like(s1h)
        for j in range(1, PEER_TOPK + 1):
            lim = jnp.where(s1h >= theta[j - 1][h:h + 1, :], float(j), lim)
        lim_ref[rows, :] = _dup_bf16(lim)
        e1_ref[rows, :] = _dup_bf16(jnp.exp(s1h - t1[0][h:h + 1, :]))
        e2_ref[rows, :] = (jnp.exp(s2_s[rows, :] - t2[0][h:h + 1, :]) * zinv[h:h + 1, :]).astype(e2_ref.dtype)


def _route(h2, wq_bf, k1_bd, k2_bd):
    B, S, _ = h2.shape
    tm = PEER_TOKEN_TILE
    hk = PEER_HEADS * PEER_NKEYS
    full = lambda a: pl.BlockSpec(a.shape, lambda b, i: (0,) * a.ndim)
    tr = pl.BlockSpec((None, hk, tm), lambda b, i: (b, 0, i))
    return pl.pallas_call(
        _route_kernel,
        grid=(B, S // tm),
        in_specs=[pl.BlockSpec((None, tm, D_MODEL), lambda b, i: (b, i, 0)),
                  full(wq_bf), full(k1_bd), full(k2_bd)],
        out_specs=[tr, tr, tr, tr],
        out_shape=[jax.ShapeDtypeStruct((B, hk, S), BF16), jax.ShapeDtypeStruct((B, hk, S), BF16),
                   jax.ShapeDtypeStruct((B, hk, S), jnp.int32), jax.ShapeDtypeStruct((B, hk, S), jnp.int32)],
        scratch_shapes=[pltpu.VMEM((hk, tm), F32), pltpu.VMEM((hk, tm), F32)],
        compiler_params=_params("parallel", "parallel"),
        name="peer_route",
    )(h2, wq_bf, k1_bd, k2_bd)


def _row_bf16(ref, row, cols, nrows):
    word = jnp.broadcast_to(ref[pl.ds(row, 1), cols], (nrows // 2, cols.stop - cols.start))
    return pltpu.bitcast(word, BF16)


def _peer_kernel(h2_ref, u_ref, vt_ref, r2_ref, e2_ref, lim_ref, e1_ref, x1_ref, g2_ref, nf_ref,
                 o_ref, acc_s, w_s, *, nj):
    j = pl.program_id(2)
    te, tm = u_ref.shape[0], h2_ref.shape[0]

    @pl.when(j == 0)
    def _():
        acc_s[...] = jnp.zeros_like(acc_s)

    act = _gelu(lax.dot_general(u_ref[...], h2_ref[...], (((1,), (1,)), ((), ())),
                                preferred_element_type=F32))
    for ii in range(te // PEER_NKEYS):
        i1 = j * (te // PEER_NKEYS) + ii
        for t0 in range(0, tm, PEER_LANE_CHUNK):
            cols = slice(t0, t0 + PEER_LANE_CHUNK)
            gate = jnp.zeros((PEER_NKEYS, PEER_LANE_CHUNK), BF16)
            for h in range(PEER_HEADS):
                rows = slice(h * PEER_NKEYS, (h + 1) * PEER_NKEYS)
                lim = _row_bf16(lim_ref, h * PEER_NKEYS + i1, cols, PEER_NKEYS)
                e1 = _row_bf16(e1_ref, h * PEER_NKEYS + i1, cols, PEER_NKEYS)
                picked = jnp.where(r2_ref[rows, cols] < lim, e2_ref[rows, cols], jnp.zeros((), BF16))
                gate = gate + e1 * picked
            w_s[ii * PEER_NKEYS:(ii + 1) * PEER_NKEYS, cols] = (
                gate * act[ii * PEER_NKEYS:(ii + 1) * PEER_NKEYS, cols].astype(BF16))
    acc_s[...] += jnp.dot(vt_ref[...], w_s[...], preferred_element_type=F32)

    @pl.when(j == nj - 1)
    def _():
        x2 = x1_ref[...] + g2_ref[...] * acc_s[...].T
        o_ref[...] = _rms(x2, nf_ref[...])


def _peer(h2, u_bf, vt_bf, r2, e2, lim, e1, x1, mod3, norm_f_g, b0):
    B, S, _ = h2.shape
    tm, te = PEER_TOKEN_TILE, PEER_EXPERT_TILE
    nj = PEER_EXPERTS // te
    hk = PEER_HEADS * PEER_NKEYS
    tok = lambda w: pl.BlockSpec((None, tm, w), lambda b, i, j: (b, i, 0))
    tr = pl.BlockSpec((None, hk, tm), lambda b, i, j: (b, 0, i))
    return pl.pallas_call(
        functools.partial(_peer_kernel, nj=nj),
        grid=(B, S // tm, nj),
        in_specs=[tok(D_MODEL),
                  pl.BlockSpec((te, D_MODEL), lambda b, i, j: (j, 0)),
                  pl.BlockSpec((D_MODEL, te), lambda b, i, j: (0, j)),
                  tr, tr, tr, tr, tok(D_MODEL),
                  pl.BlockSpec((None, 1, D_MODEL), lambda b, i, j: (b + b0, 0, 5)),
                  pl.BlockSpec((1, D_MODEL), lambda b, i, j: (0, 0))],
        out_specs=tok(D_MODEL),
        out_shape=jax.ShapeDtypeStruct((B, S, D_MODEL), F32),
        scratch_shapes=[pltpu.VMEM((D_MODEL, tm), F32), pltpu.VMEM((te, tm), BF16)],
        compiler_params=_params("parallel", "parallel", "arbitrary"),
        name="peer_dense",
    )(h2, u_bf, vt_bf, r2, e2, lim, e1, x1, mod3, norm_f_g)


def _block_diag(w):
    h, a, b = w.shape
    eye = jnp.eye(h, dtype=w.dtype)
    return (w[:, :, None, :] * eye[:, None, :, None]).reshape(h * a, h * b)


def _encode(x, mod3, b0, p):
    ucs, ux, gg = _inproj(x, mod3, p["norm1_g"], p["w_in"], b0)
    f = _seq_dft(ucs)
    hf = _scan(ux, p["conv_w"], p["conv_b"], p["wa"][0], p["ba"][0], p["wx"][0], p["bx"][0], p["lam"][0],
               reverse=False)
    yl = _scan(ux, p["conv_w"], p["conv_b"], p["wa"][1], p["ba"][1], p["wx"][1], p["bx"][1], p["lam"][1],
               reverse=True, hf=hf, gg=gg)
    x1, h2 = _outproj(f, yl, x, mod3, p["norm2_g"], p["fnet_w"], p["w_out"], b0)
    r2, e2, lim, e1 = _route(h2, p["wq"], p["k1"], p["k2"])
    return _peer(h2, p["u"], p["vt"], r2, e2, lim, e1, x1, mod3, p["norm_f_g"], b0)


def kernel(x_prompt, x_sample, c_prompt, c_sample, ada_w, ada_b, norm1_g, w_in, fnet_w, conv_w, conv_b, lru_wa, lru_ba, lru_wx, lru_bx, lru_lambda, w_out, norm2_g, peer_wq, peer_k1, peer_k2, peer_u, peer_v, norm_f_g):
    nb_p, nb_s = x_prompt.shape[0], x_sample.shape[0]
    assert nb_p + nb_s <= SUBLANES
    c_all = jnp.concatenate([c_prompt, c_sample], axis=0)
    c_pad = jnp.pad(c_all, ((0, SUBLANES - c_all.shape[0]), (0, 0)))
    mod = _ada(c_pad, ada_w[0], ada_b[0])
    mod3 = mod.reshape(SUBLANES, 1, 6 * D_MODEL)

    hd = PEER_HEADS * PEER_KEY_DIM
    wq = peer_wq[0].reshape(D_MODEL, PEER_HEADS, 2, PEER_KEY_DIM).transpose(0, 2, 1, 3).reshape(D_MODEL, 2 * hd)
    p = dict(
        norm1_g=norm1_g[0].reshape(1, D_MODEL),
        w_in=w_in[0].astype(BF16),
        fnet_w=_block_diag(fnet_w[0]).astype(BF16),
        conv_w=conv_w[0], conv_b=conv_b[0].reshape(1, LRU_WIDTH),
        wa=[_block_diag(lru_wa[0, d]).astype(BF16) for d in range(2)],
        wx=[_block_diag(lru_wx[0, d]).astype(BF16) for d in range(2)],
        ba=[lru_ba[0, d].reshape(1, LRU_WIDTH) for d in range(2)],
        bx=[lru_bx[0, d].reshape(1, LRU_WIDTH) for d in range(2)],
        lam=[lru_lambda[0, d].reshape(1, LRU_WIDTH) for d in range(2)],
        w_out=w_out[0].astype(BF16),
        norm2_g=norm2_g[0].reshape(1, D_MODEL),
        wq=wq.astype(BF16),
        k1=_block_diag(peer_k1[0]).astype(BF16),
        k2=_block_diag(peer_k2[0]).astype(BF16),
        u=peer_u[0].astype(BF16),
        vt=peer_v[0].T.astype(BF16),
        norm_f_g=norm_f_g.reshape(1, D_MODEL),
    )
    y_prompt = _encode(x_prompt, mod3, 0, p)
    y_sample = _encode(x_sample, mod3, nb_p, p)
    return (y_prompt, y_sample)
```

```python
import functools
import math

import numpy as np
import jax
import jax.numpy as jnp
from jax import lax
from jax.experimental import pallas as pl
from jax.experimental.pallas import tpu as pltpu

F32 = jnp.float32
BF16 = jnp.bfloat16
HIGHEST = lax.Precision.HIGHEST

D_MODEL = 1024
FNET_WIDTH = 512
LRU_WIDTH = 512
FNET_GROUP_DIM = 64
LRU_HEADS = 8
CONV_WIDTH = 4
CONV_LEFT = 2
RGLRU_C = 8.0
IN_WIDTH = FNET_WIDTH + 2 * LRU_WIDTH
PEER_HEADS = 8
PEER_NKEYS = 128
PEER_EXPERTS = PEER_NKEYS * PEER_NKEYS
PEER_KEY_DIM = 128
PEER_TOPK = 16
NORM_EPS = 1e-6

LANES = 128
SUBLANES = 8
DFT_N2 = 128
VMEM_LIMIT = 56 * 1024 * 1024

TOKEN_TILE = 512
SCAN_TILE = 1024
SCAN_SEG_PAD = 4
PEER_TOKEN_TILE = 512
PEER_EXPERT_TILE = 2048
PEER_LANE_CHUNK = 256


def _params(*sem):
    return pltpu.CompilerParams(dimension_semantics=sem, vmem_limit_bytes=VMEM_LIMIT)


def _gelu(x):
    return jax.nn.gelu(x, approximate=True)


def _rms(x, g):
    return x * lax.rsqrt(jnp.mean(x * x, axis=-1, keepdims=True) + NORM_EPS) * g


def _ada_kernel(c_ref, w_ref, b_ref, o_ref):
    c = c_ref[...]
    s = c * jax.nn.sigmoid(c)
    o_ref[...] = jnp.dot(s.astype(BF16), w_ref[...].astype(BF16), preferred_element_type=F32) + b_ref[...]


def _ada(c_pad, ada_w, ada_b):
    n = ada_w.shape[1]
    tn = 1536
    return pl.pallas_call(
        _ada_kernel,
        grid=(n // tn,),
        in_specs=[pl.BlockSpec((SUBLANES, D_MODEL), lambda j: (0, 0)),
                  pl.BlockSpec((D_MODEL, tn), lambda j: (0, j)),
                  pl.BlockSpec((1, tn), lambda j: (0, j))],
        out_specs=pl.BlockSpec((SUBLANES, tn), lambda j: (0, j)),
        out_shape=jax.ShapeDtypeStruct((SUBLANES, n), F32),
        compiler_params=_params("arbitrary"),
        name="adaln",
    )(c_pad, ada_w, ada_b.reshape(1, n))


def _inproj_kernel(x_ref, sh_ref, sc_ref, g_ref, w_ref, cs_ref, ucs_ref, ux_ref, gg_ref):
    h = _rms(x_ref[...], g_ref[...]) * (1.0 + sc_ref[...]) + sh_ref[...]
    p = jnp.dot(h.astype(BF16), w_ref[...], preferred_element_type=F32)
    for j in range(FNET_WIDTH // LANES):
        r = jnp.dot(p[:, j * LANES:(j + 1) * LANES], cs_ref[...], precision=HIGHEST,
                    preferred_element_type=F32)
        ucs_ref[:, j * LANES:(j + 1) * LANES] = r[:, :LANES]
        ucs_ref[:, FNET_WIDTH + j * LANES:FNET_WIDTH + (j + 1) * LANES] = r[:, LANES:]
    ux_ref[...] = p[:, FNET_WIDTH:FNET_WIDTH + LRU_WIDTH]
    gg_ref[...] = _gelu(p[:, FNET_WIDTH + LRU_WIDTH:])


def _channel_dft_matrix():
    c = np.arange(FNET_GROUP_DIM)
    ang = 2.0 * np.pi * np.outer(c, c) / FNET_GROUP_DIM
    scale = 1.0 / math.sqrt(FNET_GROUP_DIM)
    z = np.zeros((FNET_GROUP_DIM, FNET_GROUP_DIM))
    cc, ss = np.cos(ang) * scale, np.sin(ang) * scale
    return np.block([[cc, z, ss, z], [z, cc, z, ss]]).astype(np.float32)


def _inproj(x, mod3, norm1_g, w_in_bf, b0):
    B, S, _ = x.shape
    tm = TOKEN_TILE
    row = lambda k: pl.BlockSpec((None, 1, D_MODEL), lambda b, i, k=k: (b + b0, 0, k))
    tok = lambda w: pl.BlockSpec((None, tm, w), lambda b, i: (b, i, 0))
    full = lambda a: pl.BlockSpec(a.shape, lambda b, i: (0,) * a.ndim)
    cs = jnp.asarray(_channel_dft_matrix())
    return pl.pallas_call(
        _inproj_kernel,
        grid=(B, S // tm),
        in_specs=[tok(D_MODEL), row(0), row(1), full(norm1_g), full(w_in_bf), full(cs)],
        out_specs=[tok(2 * FNET_WIDTH), tok(LRU_WIDTH), tok(LRU_WIDTH)],
        out_shape=[jax.ShapeDtypeStruct((B, S, 2 * FNET_WIDTH), F32),
                   jax.ShapeDtypeStruct((B, S, LRU_WIDTH), F32),
                   jax.ShapeDtypeStruct((B, S, LRU_WIDTH), F32)],
        compiler_params=_params("parallel", "parallel"),
        name="inproj",
    )(x, mod3, mod3, norm1_g, w_in_bf, cs)


def _dft1_kernel(x_ref, fa_ref, fb_ref, o_ref, *, nb, n1):
    for j in range(nb):
        xc = x_ref[:, j * 2 * FNET_WIDTH:j * 2 * FNET_WIDTH + FNET_WIDTH]
        xs = x_ref[:, j * 2 * FNET_WIDTH + FNET_WIDTH:(j + 1) * 2 * FNET_WIDTH]
        pq = (jnp.dot(fa_ref[...], xc, precision=HIGHEST, preferred_element_type=F32)
              + jnp.dot(fb_ref[...], xs, precision=HIGHEST, preferred_element_type=F32))
        o_ref[0, :, j * FNET_WIDTH:(j + 1) * FNET_WIDTH] = pq[:n1]
        o_ref[1, :, j * FNET_WIDTH:(j + 1) * FNET_WIDTH] = pq[n1:]


def _dft2_kernel(pq_ref, o_ref, *, n1, n):
    k1 = pl.program_id(1)
    k2 = lax.broadcasted_iota(jnp.int32, (DFT_N2, DFT_N2), 0)
    n2 = lax.broadcasted_iota(jnp.int32, (DFT_N2, DFT_N2), 1)
    m = (n2 * (k1 + n1 * k2)) & (n - 1)
    ang = m.astype(F32) * (2.0 * math.pi / n)
    scale = 1.0 / math.sqrt(n)
    out = (jnp.dot(jnp.cos(ang) * scale, pq_ref[0], precision=HIGHEST, preferred_element_type=F32)
           + jnp.dot(jnp.sin(ang) * scale, pq_ref[1], precision=HIGHEST, preferred_element_type=F32))
    o_ref[...] = out.astype(o_ref.dtype)


def _seq_dft(ucs):
    B, S, _ = ucs.shape
    n1 = S // DFT_N2
    assert n1 * DFT_N2 == S and (S & (S - 1)) == 0 and n1 % SUBLANES == 0
    k = np.arange(n1)
    ang = 2.0 * np.pi * np.outer(k, k) / n1
    c1, s1 = np.cos(ang), np.sin(ang)
    fa = jnp.asarray(np.concatenate([c1, -s1], axis=0).astype(np.float32))
    fb = jnp.asarray(np.concatenate([-s1, -c1], axis=0).astype(np.float32))
    nb = 4
    x1 = ucs.reshape(B, n1, DFT_N2 * 2 * FNET_WIDTH)
    pq = pl.pallas_call(
        functools.partial(_dft1_kernel, nb=nb, n1=n1),
        grid=(B, DFT_N2 // nb),
        in_specs=[pl.BlockSpec((None, n1, nb * 2 * FNET_WIDTH), lambda b, j: (b, 0, j)),
                  pl.BlockSpec(fa.shape, lambda b, j: (0, 0)),
                  pl.BlockSpec(fb.shape, lambda b, j: (0, 0))],
        out_specs=pl.BlockSpec((None, 2, n1, nb * FNET_WIDTH), lambda b, j: (b, 0, 0, j)),
        out_shape=jax.ShapeDtypeStruct((B, 2, n1, DFT_N2 * FNET_WIDTH), F32),
        compiler_params=_params("parallel", "parallel"),
        name="seq_dft_stage1",
    )(x1, fa, fb)
    pq5 = pq.reshape(B, 2, n1, DFT_N2, FNET_WIDTH)
    f3 = pl.pallas_call(
        functools.partial(_dft2_kernel, n1=n1, n=S),
        grid=(B, n1),
        in_specs=[pl.BlockSpec((None, 2, None, DFT_N2, FNET_WIDTH), lambda b, k: (b, 0, k, 0, 0))],
        out_specs=pl.BlockSpec((None, None, DFT_N2, FNET_WIDTH), lambda b, k: (b, k, 0, 0)),
        out_shape=jax.ShapeDtypeStruct((B, n1, DFT_N2, FNET_WIDTH), BF16),
        compiler_params=_params("parallel", "parallel"),
        name="seq_dft_stage2",
    )(pq5)
    return jnp.swapaxes(f3, 1, 2).reshape(B, S, FNET_WIDTH)


def _scan_kernel(*refs, reverse, tile, nt):
    if reverse:
        (u_ref, prev_ref, next_ref, cw_ref, cb_ref, wa_ref, ba_ref, wx_ref, bx_ref, lam_ref,
         hf_ref, gg_ref, o_ref, ext_s, a_s, b_s, h_s, p_s, carry_s) = refs
    else:
        (u_ref, prev_ref, next_ref, cw_ref, cb_ref, wa_ref, ba_ref, wx_ref, bx_ref, lam_ref,
         o_ref, ext_s, a_s, b_s, h_s, p_s, carry_s) = refs
    i = pl.program_id(1)
    ti = nt - 1 - i if reverse else i
    seg = tile // SUBLANES
    pitch = seg + SCAN_SEG_PAD
    nslab = LRU_WIDTH // LANES

    @pl.when(i == 0)
    def _():
        carry_s[...] = jnp.zeros_like(carry_s)

    ext_s[0:SUBLANES, :] = jnp.where(ti == 0, 0.0, prev_ref[...])
    ext_s[SUBLANES:SUBLANES + tile, :] = u_ref[...]
    ext_s[SUBLANES + tile:, :] = jnp.where(ti == nt - 1, 0.0, next_ref[...])
    xr = cb_ref[...] + ext_s[pl.ds(SUBLANES - CONV_LEFT, tile), :] * cw_ref[0:1, :]
    for k in range(1, CONV_WIDTH):
        xr = xr + ext_s[pl.ds(SUBLANES - CONV_LEFT + k, tile), :] * cw_ref[k:k + 1, :]

    xb = xr.astype(BF16)
    r = jax.nn.sigmoid(jnp.dot(xb, wa_ref[...], preferred_element_type=F32) + ba_ref[...])
    g = jax.nn.sigmoid(jnp.dot(xb, wx_ref[...], preferred_element_type=F32) + bx_ref[...])
    lam = lam_ref[...]
    softplus_neg = jnp.maximum(-lam, 0.0) + jnp.log1p(jnp.exp(-jnp.abs(lam)))
    log_a = (-RGLRU_C * softplus_neg) * r
    a = jnp.exp(log_a)
    bvals = jnp.sqrt(-jnp.tanh(log_a) * (1.0 + a * a)) * g * xr

    for c in range(nslab):
        for j in range(SUBLANES):
            a_s[c, j * pitch:j * pitch + seg, :] = a[j * seg:(j + 1) * seg, c * LANES:(c + 1) * LANES]
            b_s[c, j * pitch:j * pitch + seg, :] = bvals[j * seg:(j + 1) * seg, c * LANES:(c + 1) * LANES]

    def body(step, carry):
        rr = seg - 1 - step if reverse else step
        out = []
        for c in range(nslab):
            h, p = carry[2 * c], carry[2 * c + 1]
            av = a_s[c, pl.ds(rr, SUBLANES, stride=pitch), :]
            bv = b_s[c, pl.ds(rr, SUBLANES, stride=pitch), :]
            h = av * h + bv
            p = av * p
            h_s[c, pl.ds(rr, SUBLANES, stride=pitch), :] = h
            p_s[c, pl.ds(rr, SUBLANES, stride=pitch), :] = p
            out += [h, p]
        return tuple(out)

    init = []
    for c in range(nslab):
        init += [jnp.zeros((SUBLANES, LANES), F32), jnp.ones((SUBLANES, LANES), F32)]
    fin = lax.fori_loop(0, seg, body, tuple(init), unroll=4)

    for c in range(nslab):
        hl, pf = fin[2 * c], fin[2 * c + 1]
        state = carry_s[:, c * LANES:(c + 1) * LANES]
        order = range(SUBLANES - 1, -1, -1) if reverse else range(SUBLANES)
        for j in order:
            hloc = h_s[c, j * pitch:j * pitch + seg, :]
            ploc = p_s[c, j * pitch:j * pitch + seg, :]
            hrow = hloc + ploc * state
            rows = slice(j * seg, (j + 1) * seg)
            cols = slice(c * LANES, (c + 1) * LANES)
            if reverse:
                o_ref[rows, cols] = ((hf_ref[rows, cols] + hrow) * gg_ref[rows, cols]).astype(o_ref.dtype)
            else:
                o_ref[rows, cols] = hrow
            state = hl[j:j + 1, :] + pf[j:j + 1, :] * state
        carry_s[:, c * LANES:(c + 1) * LANES] = state


def _scan(ux, conv_w, conv_b, wa, ba, wx, bx, lam, reverse, hf=None, gg=None):
    B, S, W = ux.shape
    tile = min(SCAN_TILE, S)
    nt = S // tile
    nh = tile // SUBLANES
    last = S // SUBLANES - 1
    t_of = (lambda i: nt - 1 - i) if reverse else (lambda i: i)
    tok = pl.BlockSpec((None, tile, W), lambda b, i: (b, t_of(i), 0))
    prev = pl.BlockSpec((None, SUBLANES, W), lambda b, i: (b, jnp.maximum(t_of(i) * nh - 1, 0), 0))
    nxt = pl.BlockSpec((None, SUBLANES, W), lambda b, i: (b, jnp.minimum((t_of(i) + 1) * nh, last), 0))
    full = lambda a: pl.BlockSpec(a.shape, lambda b, i: (0,) * a.ndim)
    args = [ux, ux, ux, conv_w, conv_b, wa, ba, wx, bx, lam]
    specs = [tok, prev, nxt] + [full(a) for a in args[3:]]
    if reverse:
        args += [hf, gg]
        specs += [tok, tok]
    pitch = tile // SUBLANES + SCAN_SEG_PAD
    slab = pltpu.VMEM((W // LANES, SUBLANES * pitch, LANES), F32)
    return pl.pallas_call(
        functools.partial(_scan_kernel, reverse=reverse, tile=tile, nt=nt),
        grid=(B, nt),
        in_specs=specs,
        out_specs=tok,
        out_shape=jax.ShapeDtypeStruct((B, S, W), BF16 if reverse else F32),
        scratch_shapes=[pltpu.VMEM((tile + 2 * SUBLANES, W), F32), slab, slab, slab, slab,
                        pltpu.VMEM((1, W), F32)],
        compiler_params=_params("parallel", "arbitrary"),
        name="rglru_bwd" if reverse else "rglru_fwd",
    )(*args)


def _outproj_kernel(f_ref, yl_ref, x_ref, g1_ref, sh_ref, sc_ref, n2_ref, fw_ref, wo_ref, x1_ref, h2_ref):
    yf = jnp.dot(f_ref[...], fw_ref[...], preferred_element_type=F32)
    o = (jnp.dot(yf.astype(BF16), wo_ref[0:FNET_WIDTH, :], preferred_element_type=F32)
         + jnp.dot(yl_ref[...], wo_ref[FNET_WIDTH:, :], preferred_element_type=F32))
    x1 = x_ref[...] + g1_ref[...] * o
    x1_ref[...] = x1
    h2_ref[...] = (_rms(x1, n2_ref[...]) * (1.0 + sc_ref[...]) + sh_ref[...]).astype(h2_ref.dtype)


def _outproj(f, yl, x, mod3, norm2_g, fw_bd, w_out_bf, b0):
    B, S, _ = x.shape
    tm = TOKEN_TILE
    row = lambda k: pl.BlockSpec((None, 1, D_MODEL), lambda b, i, k=k: (b + b0, 0, k))
    tok = lambda w: pl.BlockSpec((None, tm, w), lambda b, i: (b, i, 0))
    full = lambda a: pl.BlockSpec(a.shape, lambda b, i: (0,) * a.ndim)
    return pl.pallas_call(
        _outproj_kernel,
        grid=(B, S // tm),
        in_specs=[tok(FNET_WIDTH), tok(LRU_WIDTH), tok(D_MODEL), row(2), row(3), row(4),
                  full(norm2_g), full(fw_bd), full(w_out_bf)],
        out_specs=[tok(D_MODEL), tok(D_MODEL)],
        out_shape=[jax.ShapeDtypeStruct((B, S, D_MODEL), F32),
                   jax.ShapeDtypeStruct((B, S, D_MODEL), BF16)],
        compiler_params=_params("parallel", "parallel"),
        name="outproj",
    )(f, yl, x, mod3, mod3, mod3, norm2_g, fw_bd, w_out_bf)


def _route_kernel(h2_ref, wq_ref, k1_ref, k2_ref, r2_ref, e2_ref, lim_ref, e1_ref, s1_s, s2_s):
    q = jnp.dot(h2_ref[...], wq_ref[...], preferred_element_type=F32).astype(BF16)
    half = PEER_HEADS * PEER_KEY_DIM
    nt_dims = (((1,), (1,)), ((), ()))
    s1_s[...] = lax.dot_general(k1_ref[...], q[:, :half], nt_dims, preferred_element_type=F32)
    s2_s[...] = lax.dot_general(k2_ref[...], q[:, half:], nt_dims, preferred_element_type=F32)
    tm = h2_ref.shape[0]
    head_rows = lambda h: slice(h * PEER_NKEYS, (h + 1) * PEER_NKEYS)

    def top16(s_ref, rank_ref):
        per_head = []
        for h in range(PEER_HEADS):
            sh = s_ref[head_rows(h), :]
            prev = jnp.max(sh, axis=0, keepdims=True)
            vals = [prev]
            rank = jnp.zeros_like(sh)
            for k in range(1, PEER_TOPK + 1):
                below = sh < prev
                if rank_ref is not None:
                    rank = jnp.where(below, float(k), rank)
                if k < PEER_TOPK:
                    prev = jnp.max(jnp.where(below, sh, -jnp.inf), axis=0, keepdims=True)
                    vals.append(prev)
            if rank_ref is not None:
                rank_ref[head_rows(h), :] = rank.astype(rank_ref.dtype)
            per_head.append(vals)
        return [jnp.concatenate([per_head[h][k] for h in range(PEER_HEADS)], axis=0) for k in range(PEER_TOPK)]

    t1 = top16(s1_s, None)
    t2 = top16(s2_s, r2_ref)
    nb = [PEER_TOPK // (a + 1) for a in range(PEER_TOPK)]
    cands = [[t1[a] + t2[b] for b in range(nb[a])] for a in range(PEER_TOPK)]
    prev = cands[0][0]
    tops = [prev]
    for _ in range(1, PEER_TOPK):
        m = jnp.full((PEER_HEADS, tm), -jnp.inf, F32)
        for row in cands:
            for c in row:
                m = jnp.maximum(m, jnp.where(c < prev, c, -jnp.inf))
        tops.append(m)
        prev = m
    tau = tops[PEER_TOPK - 1]
    z = jnp.ones((PEER_HEADS, tm), F32)
    for k in range(1, PEER_TOPK):
        z = z + jnp.exp(tops[k] - tops[0])
    zinv = 1.0 / z
    big_l = []
    for a in range(PEER_TOPK):
        cnt = jnp.zeros((PEER_HEADS, tm), F32)
        for c in cands[a]:
            cnt = cnt + jnp.where(c >= tau, 1.0, 0.0)
        big_l.append(cnt)
    theta = []
    for j in range(1, PEER_TOPK + 1):
        th = jnp.full((PEER_HEADS, tm), jnp.inf, F32)
        for a in range(PEER_TOPK):
            if nb[a] >= j:
                th = jnp.minimum(th, jnp.where(big_l[a] >= float(j), t1[a], jnp.inf))
        theta.append(th)
    for h in range(PEER_HEADS):
        rows = head_rows(h)
        s1h = s1_s[rows, :]
        lim = jnp.zeros_like(s1h)
        for j in range(1, PEER_TOPK + 1):
            lim = jnp.where(s1h >= theta[j - 1][h:h + 1, :], float(j), lim)
        e1 = jnp.exp(s1h - t1[0][h:h + 1, :])
        for c in range(tm // LANES):
            lim_ref[c, rows, :] = lim[:, c * LANES:(c + 1) * LANES]
            e1_ref[c, rows, :] = e1[:, c * LANES:(c + 1) * LANES]
        e2_ref[rows, :] = (jnp.exp(s2_s[rows, :] - t2[0][h:h + 1, :]) * zinv[h:h + 1, :]).astype(e2_ref.dtype)


def _route(h2, wq_bf, k1_bd, k2_bd):
    B, S, _ = h2.shape
    tm = PEER_TOKEN_TILE
    hk = PEER_HEADS * PEER_NKEYS
    full = lambda a: pl.BlockSpec(a.shape, lambda b, i: (0,) * a.ndim)
    tr = pl.BlockSpec((None, hk, tm), lambda b, i: (b, 0, i))
    slab = pl.BlockSpec((None, tm // LANES, hk, LANES), lambda b, i: (b, i, 0, 0))
    return pl.pallas_call(
        _route_kernel,
        grid=(B, S // tm),
        in_specs=[pl.BlockSpec((None, tm, D_MODEL), lambda b, i: (b, i, 0)),
                  full(wq_bf), full(k1_bd), full(k2_bd)],
        out_specs=[tr, tr, slab, slab],
        out_shape=[jax.ShapeDtypeStruct((B, hk, S), BF16), jax.ShapeDtypeStruct((B, hk, S), BF16),
                   jax.ShapeDtypeStruct((B, S // LANES, hk, LANES), F32),
                   jax.ShapeDtypeStruct((B, S // LANES, hk, LANES), F32)],
        scratch_shapes=[pltpu.VMEM((hk, tm), F32), pltpu.VMEM((hk, tm), F32)],
        compiler_params=_params("parallel", "parallel"),
        name="peer_route",
    )(h2, wq_bf, k1_bd, k2_bd)


def _row_bf16(ref, chunk0, row, nrows, nchunks):
    parts = [jnp.broadcast_to(ref[chunk0 + k, pl.ds(row, 1), :], (nrows, LANES)) for k in range(nchunks)]
    return jnp.concatenate(parts, axis=1).astype(BF16)


def _peer_kernel(h2_ref, u_ref, vt_ref, r2_ref, e2_ref, lim_ref, e1_ref, x1_ref, g2_ref, nf_ref,
                 o_ref, acc_s, w_s, act_s, *, nj):
    j = pl.program_id(2)
    te, tm = u_ref.shape[0], h2_ref.shape[0]

    @pl.when(j == 0)
    def _():
        acc_s[...] = jnp.zeros_like(acc_s)

    act_s[...] = _gelu(lax.dot_general(u_ref[...], h2_ref[...], (((1,), (1,)), ((), ())),
                                       preferred_element_type=F32).astype(BF16))
    nck = PEER_LANE_CHUNK // LANES
    for ii in range(te // PEER_NKEYS):
        i1 = j * (te // PEER_NKEYS) + ii
        for c in range(tm // PEER_LANE_CHUNK):
            cols = slice(c * PEER_LANE_CHUNK, (c + 1) * PEER_LANE_CHUNK)
            gate = jnp.zeros((PEER_NKEYS, PEER_LANE_CHUNK), BF16)
            for h in range(PEER_HEADS):
                rows = slice(h * PEER_NKEYS, (h + 1) * PEER_NKEYS)
                lim = _row_bf16(lim_ref, c * nck, h * PEER_NKEYS + i1, PEER_NKEYS, nck)
                e1 = _row_bf16(e1_ref, c * nck, h * PEER_NKEYS + i1, PEER_NKEYS, nck)
                picked = jnp.where(r2_ref[rows, cols] < lim, e1, jnp.zeros_like(e1))
                gate = gate + picked * e2_ref[rows, cols]
            w_s[ii * PEER_NKEYS:(ii + 1) * PEER_NKEYS, cols] = (
                gate * act_s[ii * PEER_NKEYS:(ii + 1) * PEER_NKEYS, cols])
    acc_s[...] += jnp.dot(vt_ref[...], w_s[...], preferred_element_type=F32)

    @pl.when(j == nj - 1)
    def _():
        x2 = x1_ref[...] + g2_ref[...] * acc_s[...].T
        o_ref[...] = _rms(x2, nf_ref[...])


def _peer(h2, u_bf, vt_bf, r2, e2, lim, e1, x1, mod3, norm_f_g, b0):
    B, S, _ = h2.shape
    tm, te = PEER_TOKEN_TILE, PEER_EXPERT_TILE
    nj = PEER_EXPERTS // te
    hk = PEER_HEADS * PEER_NKEYS
    tok = lambda w: pl.BlockSpec((None, tm, w), lambda b, i, j: (b, i, 0))
    tr = pl.BlockSpec((None, hk, tm), lambda b, i, j: (b, 0, i))
    slab = pl.BlockSpec((None, tm // LANES, hk, LANES), lambda b, i, j: (b, i, 0, 0))
    return pl.pallas_call(
        functools.partial(_peer_kernel, nj=nj),
        grid=(B, S // tm, nj),
        in_specs=[tok(D_MODEL),
                  pl.BlockSpec((te, D_MODEL), lambda b, i, j: (j, 0)),
                  pl.BlockSpec((D_MODEL, te), lambda b, i, j: (0, j)),
                  tr, tr, slab, slab, tok(D_MODEL),
                  pl.BlockSpec((None, 1, D_MODEL), lambda b, i, j: (b + b0, 0, 5)),
                  pl.BlockSpec((1, D_MODEL), lambda b, i, j: (0, 0))],
        out_specs=tok(D_MODEL),
        out_shape=jax.ShapeDtypeStruct((B, S, D_MODEL), F32),
        scratch_shapes=[pltpu.VMEM((D_MODEL, tm), F32), pltpu.VMEM((te, tm), BF16), pltpu.VMEM((te, tm), BF16)],
        compiler_params=_params("parallel", "parallel", "arbitrary"),
        name="peer_dense",
    )(h2, u_bf, vt_bf, r2, e2, lim, e1, x1, mod3, norm_f_g)


def _block_diag(w):
    h, a, b = w.shape
    eye = jnp.eye(h, dtype=w.dtype)
    return (w[:, :, None, :] * eye[:, None, :, None]).reshape(h * a, h * b)


def _encode(x, mod3, b0, p):
    ucs, ux, gg = _inproj(x, mod3, p["norm1_g"], p["w_in"], b0)
    f = _seq_dft(ucs)
    hf = _scan(ux, p["conv_w"], p["conv_b"], p["wa"][0], p["ba"][0], p["wx"][0], p["bx"][0], p["lam"][0],
               reverse=False)
    yl = _scan(ux, p["conv_w"], p["conv_b"], p["wa"][1], p["ba"][1], p["wx"][1], p["bx"][1], p["lam"][1],
               reverse=True, hf=hf, gg=gg)
    x1, h2 = _outproj(f, yl, x, mod3, p["norm2_g"], p["fnet_w"], p["w_out"], b0)
    r2, e2, lim, e1 = _route(h2, p["wq"], p["k1"], p["k2"])
    return _peer(h2, p["u"], p["vt"], r2, e2, lim, e1, x1, mod3, p["norm_f_g"], b0)


def kernel(x_prompt, x_sample, c_prompt, c_sample, ada_w, ada_b, norm1_g, w_in, fnet_w, conv_w, conv_b, lru_wa, lru_ba, lru_wx, lru_bx, lru_lambda, w_out, norm2_g, peer_wq, peer_k1, peer_k2, peer_u, peer_v, norm_f_g):
    nb_p, nb_s = x_prompt.shape[0], x_sample.shape[0]
    assert nb_p + nb_s <= SUBLANES
    c_all = jnp.concatenate([c_prompt, c_sample], axis=0)
    c_pad = jnp.pad(c_all, ((0, SUBLANES - c_all.shape[0]), (0, 0)))
    mod = _ada(c_pad, ada_w[0], ada_b[0])
    mod3 = mod.reshape(SUBLANES, 1, 6 * D_MODEL)

    hd = PEER_HEADS * PEER_KEY_DIM
    wq = peer_wq[0].reshape(D_MODEL, PEER_HEADS, 2, PEER_KEY_DIM).transpose(0, 2, 1, 3).reshape(D_MODEL, 2 * hd)
    p = dict(
        norm1_g=norm1_g[0].reshape(1, D_MODEL),
        w_in=w_in[0].astype(BF16),
        fnet_w=_block_diag(fnet_w[0]).astype(BF16),
        conv_w=conv_w[0], conv_b=conv_b[0].reshape(1, LRU_WIDTH),
        wa=[_block_diag(lru_wa[0, d]).astype(BF16) for d in range(2)],
        wx=[_block_diag(lru_wx[0, d]).astype(BF16) for d in range(2)],
        ba=[lru_ba[0, d].reshape(1, LRU_WIDTH) for d in range(2)],
        bx=[lru_bx[0, d].reshape(1, LRU_WIDTH) for d in range(2)],
        lam=[lru_lambda[0, d].reshape(1, LRU_WIDTH) for d in range(2)],
        w_out=w_out[0].astype(BF16),
        norm2_g=norm2_g[0].reshape(1, D_MODEL),
        wq=wq.astype(BF16),
        k1=_block_diag(peer_k1[0]).astype(BF16),
        k2=_block_diag(peer_k2[0]).astype(BF16),
        u=peer_u[0].astype(BF16),
        vt=peer_v[0].T.astype(BF16),
        norm_f_g=norm_f_g.reshape(1, D_MODEL),
    )
    y_prompt = _encode(x_prompt, mod3, 0, p)
    y_sample = _encode(x_sample, mod3, nb_p, p)
    return (y_prompt, y_sample)
```
